```python
import math
import jax, jax.numpy as jnp
from jax import lax
import numpy as np

D_MODEL = 1024
BATCH = 2
SEQ = 16384
DEPTH = 1

HEAD_DIM = 128
N_Q_HEADS = 8
N_KV_HEADS = 2
Q_PER_KV = N_Q_HEADS // N_KV_HEADS
ATTN_WIDTH = N_Q_HEADS * HEAD_DIM
KV_WIDTH = N_KV_HEADS * HEAD_DIM
FOURIER_GROUPS = 4
FOURIER_GROUP_DIM = 128
FOURIER_WIDTH = FOURIER_GROUPS * FOURIER_GROUP_DIM
IN_WIDTH = ATTN_WIDTH + 2 * KV_WIDTH + ATTN_WIDTH + 2 * FOURIER_WIDTH
GRID_W = 64
AXIS_DIM = HEAD_DIM // 2
ROPE_THETA = 10000.0
Q_BLOCK = 128
NORM_EPS = 1e-6

kernel_name = "hybrid_axial_gqa_fourier_gated_merge"


def rms_norm(x, g):
    xf = x.astype(jnp.float32)
    y = xf * lax.rsqrt(jnp.mean(xf * xf, axis=-1, keepdims=True) + NORM_EPS)
    return (y * g.astype(jnp.float32)).astype(x.dtype)


def axial_rope_tables(seq_len):
    rows = seq_len // GRID_W
    row = jnp.repeat(jnp.arange(rows, dtype=jnp.float32), GRID_W)
    col = jnp.tile(jnp.arange(GRID_W, dtype=jnp.float32), rows)
    inv = ROPE_THETA ** (-jnp.arange(0, AXIS_DIM, 2, dtype=jnp.float32) / AXIS_DIM)
    ang_r = row[:, None] * inv[None, :]
    ang_c = col[:, None] * inv[None, :]
    ang_r = jnp.concatenate([ang_r, ang_r], axis=-1)
    ang_c = jnp.concatenate([ang_c, ang_c], axis=-1)
    return jnp.cos(ang_r), jnp.sin(ang_r), jnp.cos(ang_c), jnp.sin(ang_c)


def rotate_half_apply(x, cos, sin):
    x1, x2 = jnp.split(x, 2, axis=-1)
    rot = jnp.concatenate([-x2, x1], axis=-1)
    return x * cos[:, None, :] + rot * sin[:, None, :]


def axial_rope(x, tables):
    cos_r, sin_r, cos_c, sin_c = tables
    xf = x.astype(jnp.float32)
    xr, xc = jnp.split(xf, 2, axis=-1)
    out = jnp.concatenate([rotate_half_apply(xr, cos_r, sin_r),
                           rotate_half_apply(xc, cos_c, sin_c)], axis=-1)
    return out.astype(x.dtype)


def bidirectional_gqa(q, k, v):
    b, s = q.shape[0], q.shape[1]
    nb = s // Q_BLOCK
    scale = 1.0 / math.sqrt(HEAD_DIM)
    qb = q.reshape(b, nb, Q_BLOCK, N_KV_HEADS, Q_PER_KV, HEAD_DIM).transpose(1, 0, 2, 3, 4, 5)

    def one_block(qi):
        sc = jnp.einsum('bqhgd,bkhd->bhgqk', qi, k).astype(jnp.float32) * scale
        p = jax.nn.softmax(sc, axis=-1)
        return jnp.einsum('bhgqk,bkhd->bqhgd', p.astype(v.dtype), v)

    o = lax.map(one_block, qb)
    return o.transpose(1, 0, 2, 3, 4, 5).reshape(b, s, ATTN_WIDTH)


def fourier_mix(u):
    b, s, _ = u.shape
    ug = u.astype(jnp.float32).reshape(b, s, FOURIER_GROUPS, FOURIER_GROUP_DIM)
    f = jnp.fft.fft2(ug, axes=(1, 3), norm='ortho').real
    return f.reshape(b, s, FOURIER_WIDTH).astype(u.dtype)


def setup_inputs(seed: int = 0) -> dict:
    key = jax.random.key(seed)
    ks = jax.random.split(key, 11)
    f32 = jnp.float32
    x = jax.random.normal(ks[0], (BATCH, SEQ, D_MODEL), f32)
    norm_g = 1.0 + 0.02 * jax.random.normal(ks[1], (DEPTH, D_MODEL), f32)
    w_in = jax.random.normal(ks[2], (DEPTH, D_MODEL, IN_WIDTH), f32) * D_MODEL ** -0.5
    q_norm_g = 1.0 + 0.02 * jax.random.normal(ks[3], (DEPTH, HEAD_DIM), f32)
    k_norm_g = 1.0 + 0.02 * jax.random.normal(ks[4], (DEPTH, HEAD_DIM), f32)
    w_attn_proj = jax.random.normal(ks[5], (DEPTH, ATTN_WIDTH, D_MODEL), f32) * ATTN_WIDTH ** -0.5
    w_fourier_proj = jax.random.normal(ks[6], (DEPTH, FOURIER_WIDTH, D_MODEL), f32) * FOURIER_WIDTH ** -0.5
    w_merge = jax.random.normal(ks[7], (DEPTH, D_MODEL, 2 * D_MODEL), f32) * D_MODEL ** -0.5
    b_merge = 0.01 * jax.random.normal(ks[8], (DEPTH, 2 * D_MODEL), f32)
    w_out = jax.random.normal(ks[9], (DEPTH, D_MODEL, D_MODEL), f32) * D_MODEL ** -0.5
    return {"x": x, "norm_g": norm_g, "w_in": w_in, "q_norm_g": q_norm_g,
            "k_norm_g": k_norm_g, "w_attn_proj": w_attn_proj,
            "w_fourier_proj": w_fourier_proj, "w_merge": w_merge,
            "b_merge": b_merge, "w_out": w_out}


def reference(x, norm_g, w_in, q_norm_g, k_norm_g, w_attn_proj, w_fourier_proj,
              w_merge, b_merge, w_out):
    b, s, _ = x.shape
    tables = axial_rope_tables(s)
    splits = np.cumsum([ATTN_WIDTH, KV_WIDTH, KV_WIDTH, ATTN_WIDTH, FOURIER_WIDTH]).tolist()
    for l in range(DEPTH):
        h = rms_norm(x, norm_g[l])
        proj = jnp.einsum('bsd,de->bse', h, w_in[l])
        q, k, v, z_attn, u_f, z_f = jnp.split(proj, splits, axis=-1)

        q = rms_norm(q.reshape(b, s, N_Q_HEADS, HEAD_DIM), q_norm_g[l])
        k = rms_norm(k.reshape(b, s, N_KV_HEADS, HEAD_DIM), k_norm_g[l])
        v = v.reshape(b, s, N_KV_HEADS, HEAD_DIM)
        q = axial_rope(q, tables).reshape(b, s, N_KV_HEADS, Q_PER_KV, HEAD_DIM)
        k = axial_rope(k, tables)
        a = bidirectional_gqa(q, k, v) * jax.nn.silu(z_attn)
        y_attn = jnp.einsum('bse,ed->bsd', a, w_attn_proj[l])

        f = fourier_mix(u_f) * jax.nn.silu(z_f)
        y_four = jnp.einsum('bse,ed->bsd', f, w_fourier_proj[l])

        gates = jax.nn.sigmoid(jnp.einsum('bsd,de->bse', h, w_merge[l]) + b_merge[l])
        g_attn, g_four = jnp.split(gates, 2, axis=-1)
        merged = g_attn * y_attn + g_four * y_four
        x = x + jnp.einsum('bsd,de->bse', merged, w_out[l])
    return x
```

```python
import functools
import math

import numpy as np
import jax
import jax.numpy as jnp
from jax import lax
from jax.experimental import pallas as pl
from jax.experimental.pallas import tpu as pltpu

HEAD_DIM = 128
N_Q_HEADS = 8
N_KV_HEADS = 2
Q_PER_KV = N_Q_HEADS // N_KV_HEADS
ATTN_WIDTH = N_Q_HEADS * HEAD_DIM
KV_WIDTH = N_KV_HEADS * HEAD_DIM
FOURIER_GROUPS = 4
FOURIER_GROUP_DIM = 128
FOURIER_WIDTH = FOURIER_GROUPS * FOURIER_GROUP_DIM
GRID_W = 64
AXIS_DIM = HEAD_DIM // 2
ROPE_THETA = 10000.0
NORM_EPS = 1e-6

DFT_RADIX = 128
ROW_TILE = 512
Q_TILE = 256
KV_TILE = 512
FFT_GROUP = 8
VMEM_LIMIT = 56 * 1024 * 1024

_BF16 = jnp.bfloat16
_F32 = jnp.float32
_NT = (((1,), (1,)), ((), ()))


def _rope_tables(seq_len):
    t = np.arange(seq_len)
    row = (t // GRID_W).astype(np.float32)
    col = (t % GRID_W).astype(np.float32)
    inv = (np.float32(ROPE_THETA) ** (-np.arange(0, AXIS_DIM, 2, dtype=np.float32) / np.float32(AXIS_DIM)))
    inv = inv.astype(np.float32)
    ang_r = (row[:, None] * inv[None, :]).astype(np.float32).astype(np.float64)
    ang_c = (col[:, None] * inv[None, :]).astype(np.float32).astype(np.float64)
    ang = np.concatenate([ang_r, ang_r, ang_c, ang_c], axis=-1)
    sign = np.tile(np.concatenate([-np.ones(AXIS_DIM // 2), np.ones(AXIS_DIM // 2)]), 2)
    return np.cos(ang).astype(np.float32), (np.sin(ang) * sign[None, :]).astype(np.float32)


def _dft_tables(seq_len):
    n1 = seq_len // DFT_RADIX
    n2 = DFT_RADIX
    a = np.arange(n1)
    ang = 2.0 * np.pi * ((a[:, None] * a[None, :]) % n1) / n1
    cs1 = np.concatenate([np.cos(ang), np.sin(ang)], axis=0) / math.sqrt(n1)
    c = np.arange(FOURIER_GROUP_DIM)
    angc = 2.0 * np.pi * ((c[:, None] * c[None, :]) % FOURIER_GROUP_DIM) / FOURIER_GROUP_DIM
    cc, sc = np.cos(angc), np.sin(angc)
    m2 = np.block([[cc, -sc], [-sc, -cc]]) / math.sqrt(FOURIER_GROUP_DIM)
    k1 = np.arange(n1)[:, None, None]
    k2 = np.arange(n2)[None, :, None]
    nn = np.arange(n2)[None, None, :]
    angg = 2.0 * np.pi * ((nn * (k1 + n1 * k2)) % seq_len) / seq_len
    g = np.concatenate([np.cos(angg), np.sin(angg)], axis=-1) / math.sqrt(n2)
    to_bf16 = lambda v: jnp.asarray(v.astype(np.float32)).astype(_BF16)
    return to_bf16(cs1), to_bf16(m2), to_bf16(g)


def _proj_kernel(x_ref, g_ref, w_ref, wvt_ref, qg_ref, kg_ref, cos_ref, sin_ref,
                 q_out, k_out, vt_out, sa_out, u_out, sf_out):
    x = x_ref[...]
    ms = jnp.mean(x * x, axis=-1, keepdims=True)
    h = (x * lax.rsqrt(ms + NORM_EPS) * g_ref[...]).astype(_BF16)

    cos = cos_ref[...]
    sin = sin_ref[...]
    lane = lax.broadcasted_iota(jnp.int32, (1, HEAD_DIM), 1)
    first_half = (lane % AXIS_DIM) < (AXIS_DIM // 2)

    def norm_rope(xh, gain):
        y = xh * lax.rsqrt(jnp.mean(xh * xh, axis=-1, keepdims=True) + NORM_EPS) * gain
        up = pltpu.roll(y, HEAD_DIM - AXIS_DIM // 2, 1)
        down = pltpu.roll(y, AXIS_DIM // 2, 1)
        return y * cos + jnp.where(first_half, up, down) * sin

    q_scale = math.log2(math.e) / math.sqrt(HEAD_DIM)
    o = 0
    pq = jnp.dot(h, w_ref[:, o:o + ATTN_WIDTH], preferred_element_type=_F32)
    for hh in range(N_Q_HEADS):
        sl = slice(hh * HEAD_DIM, (hh + 1) * HEAD_DIM)
        q_out[:, sl] = (norm_rope(pq[:, sl], qg_ref[...]) * q_scale).astype(_BF16)
    o += ATTN_WIDTH
    pk = jnp.dot(h, w_ref[:, o:o + KV_WIDTH], preferred_element_type=_F32)
    for hh in range(N_KV_HEADS):
        sl = slice(hh * HEAD_DIM, (hh + 1) * HEAD_DIM)
        k_out[:, sl] = norm_rope(pk[:, sl], kg_ref[...]).astype(_BF16)
    o += 2 * KV_WIDTH
    vt_out[...] = lax.dot_general(wvt_ref[...], h, _NT, preferred_element_type=_F32).astype(_BF16)
    za = jnp.dot(h, w_ref[:, o:o + ATTN_WIDTH], preferred_element_type=_F32)
    sa_out[...] = (za * jax.nn.sigmoid(za)).astype(_BF16)
    o += ATTN_WIDTH
    u_out[...] = jnp.dot(h, w_ref[:, o:o + FOURIER_WIDTH], preferred_element_type=_F32)
    o += FOURIER_WIDTH
    zf = jnp.dot(h, w_ref[:, o:o + FOURIER_WIDTH], preferred_element_type=_F32)
    sf_out[...] = (zf * jax.nn.sigmoid(zf)).astype(_BF16)


def _proj(x, norm_g, w_in, q_norm_g, k_norm_g):
    b, s, d = x.shape
    r = min(ROW_TILE, s)
    in_w = w_in.shape[1]
    cos, sin = _rope_tables(s)
    w = w_in.astype(_BF16)
    wvt = w_in[:, ATTN_WIDTH + KV_WIDTH:ATTN_WIDTH + 2 * KV_WIDTH].T.astype(_BF16)
    row = lambda width: pl.BlockSpec((None, r, width), lambda bi, i: (bi, i, 0))
    full = lambda shape: pl.BlockSpec(shape, lambda bi, i: (0,) * len(shape))
    tab = pl.BlockSpec((r, HEAD_DIM), lambda bi, i: (i, 0))
    return pl.pallas_call(
        _proj_kernel,
        grid=(b, s // r),
        in_specs=[row(d), full((1, d)), full((d, in_w)), full((KV_WIDTH, d)),
                  full((1, HEAD_DIM)), full((1, HEAD_DIM)), tab, tab],
        out_specs=[row(ATTN_WIDTH), row(KV_WIDTH),
                   pl.BlockSpec((None, KV_WIDTH, r), lambda bi, i: (bi, 0, i)),
                   row(ATTN_WIDTH), row(FOURIER_WIDTH), row(FOURIER_WIDTH)],
        out_shape=[jax.ShapeDtypeStruct((b, s, ATTN_WIDTH), _BF16),
                   jax.ShapeDtypeStruct((b, s, KV_WIDTH), _BF16),
                   jax.ShapeDtypeStruct((b, KV_WIDTH, s), _BF16),
                   jax.ShapeDtypeStruct((b, s, ATTN_WIDTH), _BF16),
                   jax.ShapeDtypeStruct((b, s, FOURIER_WIDTH), _F32),
                   jax.ShapeDtypeStruct((b, s, FOURIER_WIDTH), _BF16)],
        compiler_params=pltpu.CompilerParams(
            dimension_semantics=("parallel", "parallel"), vmem_limit_bytes=VMEM_LIMIT),
        name="proj",
    )(x, norm_g.reshape(1, d), w, wvt, q_norm_g.reshape(1, HEAD_DIM), k_norm_g.reshape(1, HEAD_DIM),
      jnp.asarray(cos), jnp.asarray(sin))


def _attn_kernel(q_ref, k_ref, vt_ref, o_ref, acc_ref, *, kv_tile):
    tq = q_ref.shape[0]
    n_kv = k_ref.shape[0] // kv_tile
    acc_ref[...] = jnp.zeros_like(acc_ref)

    def body(j, carry):
        ms, ls = carry
        start = pl.multiple_of(j * kv_tile, kv_tile)
        k_c = k_ref[pl.ds(start, kv_tile), :]
        vt_c = vt_ref[:, pl.ds(start, kv_tile)]
        new_ms, new_ls = [], []
        for hh in range(Q_PER_KV):
            q_h = q_ref[:, hh * HEAD_DIM:(hh + 1) * HEAD_DIM]
            s = lax.dot_general(k_c, q_h, _NT, preferred_element_type=_F32)
            m_new = jnp.maximum(ms[hh], jnp.max(s, axis=0, keepdims=True))
            alpha = jnp.exp2(ms[hh] - m_new)
            p = jnp.exp2(s - m_new)
            new_ls.append(alpha * ls[hh] + jnp.sum(p, axis=0, keepdims=True))
            new_ms.append(m_new)
            pv = jnp.dot(vt_c, p.astype(_BF16), preferred_element_type=_F32)
            acc_ref[hh] = alpha * acc_ref[hh] + pv
        return tuple(new_ms), tuple(new_ls)

    m0 = tuple(jnp.full((1, tq), -1e30, _F32) for _ in range(Q_PER_KV))
    l0 = tuple(jnp.zeros((1, tq), _F32) for _ in range(Q_PER_KV))
    _, ls = lax.fori_loop(0, n_kv, body, (m0, l0))
    for hh in range(Q_PER_KV):
        o_t = acc_ref[hh] / ls[hh]
        o_ref[:, hh * HEAD_DIM:(hh + 1) * HEAD_DIM] = o_t.T.astype(o_ref.dtype)


def _attention(q, k, vt):
    b, s, _ = q.shape
    tq = min(Q_TILE, s)
    tk = min(KV_TILE, s)
    gw = Q_PER_KV * HEAD_DIM
    return pl.pallas_call(
        functools.partial(_attn_kernel, kv_tile=tk),
        grid=(b, N_KV_HEADS, s // tq),
        in_specs=[pl.BlockSpec((None, tq, gw), lambda bi, g, i: (bi, i, g)),
                  pl.BlockSpec((None, s, HEAD_DIM), lambda bi, g, i: (bi, 0, g)),
                  pl.BlockSpec((None, HEAD_DIM, s), lambda bi, g, i: (bi, g, 0))],
        out_specs=pl.BlockSpec((None, tq, gw), lambda bi, g, i: (bi, i, g)),
        out_shape=jax.ShapeDtypeStruct((b, s, ATTN_WIDTH), _BF16),
        scratch_shapes=[pltpu.VMEM((Q_PER_KV, HEAD_DIM, tq), _F32)],
        compiler_params=pltpu.CompilerParams(
            dimension_semantics=("parallel", "parallel", "parallel"), vmem_limit_bytes=VMEM_LIMIT),
        name="attn",
    )(q, k, vt)


def _fft_a_kernel(u_ref, cs1_ref, m2_ref, y_ref):
    n1 = u_ref.shape[0]
    gd = FOURIER_GROUP_DIM
    for jj in range(u_ref.shape[1]):
        uj = u_ref[:, jj, :].astype(_BF16)
        pq = jnp.dot(cs1_ref[...], uj, preferred_element_type=_F32)
        for g in range(FOURIER_GROUPS):
            sl = slice(g * gd, (g + 1) * gd)
            lhs = jnp.concatenate([pq[:n1, sl], pq[n1:, sl]], axis=1).astype(_BF16)
            yy = jnp.dot(lhs, m2_ref[...], preferred_element_type=_F32)
            y_ref[:, 0, jj, sl] = yy[:, :gd]
            y_ref[:, 1, jj, sl] = yy[:, gd:]


def _fft_b_kernel(y_ref, g_ref, x_ref):
    for kk in range(y_ref.shape[0]):
        yk = y_ref[kk].astype(_BF16)
        x_ref[:, kk, :] = jnp.dot(g_ref[kk], yk, preferred_element_type=_F32)


def _fourier_mix(u):
    b, s, w = u.shape
    n2 = DFT_RADIX
    n1 = s // n2
    cs1, m2, g = _dft_tables(s)
    ga = min(FFT_GROUP, n2)
    gb = min(FFT_GROUP, n1)
    cparams = pltpu.CompilerParams(dimension_semantics=("parallel", "parallel"),
                                   vmem_limit_bytes=VMEM_LIMIT)
    y = pl.pallas_call(
        _fft_a_kernel,
        grid=(b, n2 // ga),
        in_specs=[pl.BlockSpec((None, n1, ga, w), lambda bi, j: (bi, 0, j, 0)),
                  pl.BlockSpec((2 * n1, n1), lambda bi, j: (0, 0)),
                  pl.BlockSpec((2 * FOURIER_GROUP_DIM, 2 * FOURIER_GROUP_DIM), lambda bi, j: (0, 0))],
        out_specs=pl.BlockSpec((None, n1, 2, ga, w), lambda bi, j: (bi, 0, 0, j, 0)),
        out_shape=jax.ShapeDtypeStruct((b, n1, 2, n2, w), _F32),
        compiler_params=cparams,
        name="fft_a",
    )(u.reshape(b, n1, n2, w), cs1, m2)
    x = pl.pallas_call(
        _fft_b_kernel,
        grid=(b, n1 // gb),
        in_specs=[pl.BlockSpec((None, gb, 2 * n2, w), lambda bi, j: (bi, j, 0, 0)),
                  pl.BlockSpec((gb, n2, 2 * n2), lambda bi, j: (j, 0, 0))],
        out_specs=pl.BlockSpec((None, n2, gb, w), lambda bi, j: (bi, 0, j, 0)),
        out_shape=jax.ShapeDtypeStruct((b, n2, n1, w), _F32),
        compiler_params=cparams,
        name="fft_b",
    )(y.reshape(b, n1, 2 * n2, w), g)
    return x.reshape(b, s, w)


def _merge_kernel(x_ref, o_ref, sa_ref, f_ref, sf_ref, g_ref, wap_ref, wfp_ref, wm_ref, bm_ref,
                  wo_ref, out_ref):
    x = x_ref[...]
    d = x.shape[-1]
    ms = jnp.mean(x * x, axis=-1, keepdims=True)
    h = (x * lax.rsqrt(ms + NORM_EPS) * g_ref[...]).astype(_BF16)
    a = o_ref[...] * sa_ref[...]
    y_attn = jnp.dot(a, wap_ref[...], preferred_element_type=_F32)
    f = (f_ref[...] * sf_ref[...].astype(_F32)).astype(_BF16)
    y_four = jnp.dot(f, wfp_ref[...], preferred_element_type=_F32)
    gates = jax.nn.sigmoid(jnp.dot(h, wm_ref[...], preferred_element_type=_F32) + bm_ref[...])
    merged = (gates[:, :d] * y_attn + gates[:, d:] * y_four).astype(_BF16)
    out_ref[...] = x + jnp.dot(merged, wo_ref[...], preferred_element_type=_F32)


def _merge(x, o, sa, f, sf, norm_g, w_attn_proj, w_fourier_proj, w_merge, b_merge, w_out):
    b, s, d = x.shape
    r = min(ROW_TILE, s)
    row = lambda width: pl.BlockSpec((None, r, width), lambda bi, i: (bi, i, 0))
    full = lambda shape: pl.BlockSpec(shape, lambda bi, i: (0,) * len(shape))
    return pl.pallas_call(
        _merge_kernel,
        grid=(b, s // r),
        in_specs=[row(d), row(ATTN_WIDTH), row(ATTN_WIDTH), row(FOURIER_WIDTH), row(FOURIER_WIDTH),
                  full((1, d)), full((ATTN_WIDTH, d)), full((FOURIER_WIDTH, d)), full((d, 2 * d)),
                  full((1, 2 * d)), full((d, d))],
        out_specs=row(d),
        out_shape=jax.ShapeDtypeStruct((b, s, d), x.dtype),
        compiler_params=pltpu.CompilerParams(
            dimension_semantics=("parallel", "parallel"), vmem_limit_bytes=VMEM_LIMIT),
        name="merge",
    )(x, o, sa, f, sf, norm_g.reshape(1, d), w_attn_proj.astype(_BF16), w_fourier_proj.astype(_BF16),
      w_merge.astype(_BF16), b_merge.reshape(1, 2 * d), w_out.astype(_BF16))


def kernel(x, norm_g, w_in, q_norm_g, k_norm_g, w_attn_proj, w_fourier_proj, w_merge, b_merge, w_out):
    depth = norm_g.shape[0]
    for l in range(depth):
        q, k, vt, sa, u, sf = _proj(x, norm_g[l], w_in[l], q_norm_g[l], k_norm_g[l])
        o = _attention(q, k, vt)
        f = _fourier_mix(u)
        x = _merge(x, o, sa, f, sf, norm_g[l], w_attn_proj[l], w_fourier_proj[l], w_merge[l],
                   b_merge[l], w_out[l])
    return x
```

```python
import functools
import math

import numpy as np
import jax
import jax.numpy as jnp
from jax import lax
from jax.experimental import pallas as pl
from jax.experimental.pallas import tpu as pltpu

HEAD_DIM = 128
N_Q_HEADS = 8
N_KV_HEADS = 2
Q_PER_KV = N_Q_HEADS // N_KV_HEADS
ATTN_WIDTH = N_Q_HEADS * HEAD_DIM
KV_WIDTH = N_KV_HEADS * HEAD_DIM
FOURIER_GROUPS = 4
FOURIER_GROUP_DIM = 128
FOURIER_WIDTH = FOURIER_GROUPS * FOURIER_GROUP_DIM
GRID_W = 64
AXIS_DIM = HEAD_DIM // 2
ROPE_THETA = 10000.0
NORM_EPS = 1e-6

DFT_RADIX = 128
ROW_TILE = 512
Q_TILE = 256
KV_TILE = 512
FFT_GROUP = 8
VMEM_LIMIT = 56 * 1024 * 1024

_BF16 = jnp.bfloat16
_F32 = jnp.float32
_NT = (((1,), (1,)), ((), ()))


def _rope_tables(seq_len):
    t = np.arange(seq_len)
    row = (t // GRID_W).astype(np.float32)
    col = (t % GRID_W).astype(np.float32)
    inv = (np.float32(ROPE_THETA) ** (-np.arange(0, AXIS_DIM, 2, dtype=np.float32) / np.float32(AXIS_DIM)))
    inv = inv.astype(np.float32)
    ang_r = (row[:, None] * inv[None, :]).astype(np.float32).astype(np.float64)
    ang_c = (col[:, None] * inv[None, :]).astype(np.float32).astype(np.float64)
    ang = np.concatenate([ang_r, ang_r, ang_c, ang_c], axis=-1)
    sign = np.tile(np.concatenate([-np.ones(AXIS_DIM // 2), np.ones(AXIS_DIM // 2)]), 2)
    return np.cos(ang).astype(np.float32), (np.sin(ang) * sign[None, :]).astype(np.float32)


def _dft_tables(seq_len):
    n1 = seq_len // DFT_RADIX
    n2 = DFT_RADIX
    a = np.arange(n1)
    ang = 2.0 * np.pi * ((a[:, None] * a[None, :]) % n1) / n1
    cs1 = np.concatenate([np.cos(ang), np.sin(ang)], axis=0) / math.sqrt(n1)
    c = np.arange(FOURIER_GROUP_DIM)
    angc = 2.0 * np.pi * ((c[:, None] * c[None, :]) % FOURIER_GROUP_DIM) / FOURIER_GROUP_DIM
    cc, sc = np.cos(angc), np.sin(angc)
    m2 = np.block([[cc, -sc], [-sc, -cc]]) / math.sqrt(FOURIER_GROUP_DIM)
    k1 = np.arange(n1)[:, None, None]
    k2 = np.arange(n2)[None, :, None]
    nn = np.arange(n2)[None, None, :]
    angg = 2.0 * np.pi * ((nn * (k1 + n1 * k2)) % seq_len) / seq_len
    g = np.concatenate([np.cos(angg), np.sin(angg)], axis=-1) / math.sqrt(n2)
    to_bf16 = lambda v: jnp.asarray(v.astype(np.float32)).astype(_BF16)
    return to_bf16(cs1), to_bf16(m2), to_bf16(g)


def _proj_kernel(x_ref, g_ref, w_ref, wvt_ref, qg_ref, kg_ref, cos_ref, sin_ref,
                 q_out, k_out, vt_out, sa_out, u_out, sf_out):
    x = x_ref[...]
    ms = jnp.mean(x * x, axis=-1, keepdims=True)
    h = (x * lax.rsqrt(ms + NORM_EPS) * g_ref[...]).astype(_BF16)

    cos = cos_ref[...]
    sin = sin_ref[...]
    lane = lax.broadcasted_iota(jnp.int32, (1, HEAD_DIM), 1)
    first_half = (lane % AXIS_DIM) < (AXIS_DIM // 2)

    def norm_rope(xh, gain):
        y = xh * lax.rsqrt(jnp.mean(xh * xh, axis=-1, keepdims=True) + NORM_EPS) * gain
        up = pltpu.roll(y, HEAD_DIM - AXIS_DIM // 2, 1)
        down = pltpu.roll(y, AXIS_DIM // 2, 1)
        return y * cos + jnp.where(first_half, up, down) * sin

    q_scale = math.log2(math.e) / math.sqrt(HEAD_DIM)
    o = 0
    pq = jnp.dot(h, w_ref[:, o:o + ATTN_WIDTH], preferred_element_type=_F32)
    for hh in range(N_Q_HEADS):
        sl = slice(hh * HEAD_DIM, (hh + 1) * HEAD_DIM)
        q_out[:, sl] = (norm_rope(pq[:, sl], qg_ref[...]) * q_scale).astype(_BF16)
    o += ATTN_WIDTH
    pk = jnp.dot(h, w_ref[:, o:o + KV_WIDTH], preferred_element_type=_F32)
    for hh in range(N_KV_HEADS):
        sl = slice(hh * HEAD_DIM, (hh + 1) * HEAD_DIM)
        k_out[:, sl] = norm_rope(pk[:, sl], kg_ref[...]).astype(_BF16)
    o += 2 * KV_WIDTH
    vt_out[...] = lax.dot_general(wvt_ref[...], h, _NT, preferred_element_type=_F32).astype(_BF16)
    za = jnp.dot(h, w_ref[:, o:o + ATTN_WIDTH], preferred_element_type=_F32)
    sa_out[...] = (za * jax.nn.sigmoid(za)).astype(_BF16)
    o += ATTN_WIDTH
    u_out[...] = jnp.dot(h, w_ref[:, o:o + FOURIER_WIDTH], preferred_element_type=_F32)
    o += FOURIER_WIDTH
    zf = jnp.dot(h, w_ref[:, o:o + FOURIER_WIDTH], preferred_element_type=_F32)
    sf_out[...] = (zf * jax.nn.sigmoid(zf)).astype(_BF16)


def _proj(x, norm_g, w_in, q_norm_g, k_norm_g):
    b, s, d = x.shape
    r = min(ROW_TILE, s)
    in_w = w_in.shape[1]
    cos, sin = _rope_tables(s)
    w = w_in.astype(_BF16)
    wvt = w_in[:, ATTN_WIDTH + KV_WIDTH:ATTN_WIDTH + 2 * KV_WIDTH].T.astype(_BF16)
    row = lambda width: pl.BlockSpec((None, r, width), lambda bi, i: (bi, i, 0))
    full = lambda shape: pl.BlockSpec(shape, lambda bi, i: (0,) * len(shape))
    tab = pl.BlockSpec((r, HEAD_DIM), lambda bi, i: (i, 0))
    return pl.pallas_call(
        _proj_kernel,
        grid=(b, s // r),
        in_specs=[row(d), full((1, d)), full((d, in_w)), full((KV_WIDTH, d)),
                  full((1, HEAD_DIM)), full((1, HEAD_DIM)), tab, tab],
        out_specs=[row(ATTN_WIDTH), row(KV_WIDTH),
                   pl.BlockSpec((None, KV_WIDTH, r), lambda bi, i: (bi, 0, i)),
                   row(ATTN_WIDTH), row(FOURIER_WIDTH), row(FOURIER_WIDTH)],
        out_shape=[jax.ShapeDtypeStruct((b, s, ATTN_WIDTH), _BF16),
                   jax.ShapeDtypeStruct((b, s, KV_WIDTH), _BF16),
                   jax.ShapeDtypeStruct((b, KV_WIDTH, s), _BF16),
                   jax.ShapeDtypeStruct((b, s, ATTN_WIDTH), _BF16),
                   jax.ShapeDtypeStruct((b, s, FOURIER_WIDTH), _F32),
                   jax.ShapeDtypeStruct((b, s, FOURIER_WIDTH), _BF16)],
        compiler_params=pltpu.CompilerParams(
            dimension_semantics=("parallel", "parallel"), vmem_limit_bytes=VMEM_LIMIT),
        name="proj",
    )(x, norm_g.reshape(1, d), w, wvt, q_norm_g.reshape(1, HEAD_DIM), k_norm_g.reshape(1, HEAD_DIM),
      jnp.asarray(cos), jnp.asarray(sin))


def _attn_kernel(q_ref, k_ref, vt_ref, o_ref, s_ref, mc_ref, acc_ref, *, kv_tile):
    tq = q_ref.shape[0]
    n_kv = k_ref.shape[0] // kv_tile
    acc_ref[...] = jnp.zeros_like(acc_ref)

    def scores(c, slot):
        start = pl.multiple_of(c * kv_tile, kv_tile)
        k_c = k_ref[pl.ds(start, kv_tile), :]
        for hh in range(Q_PER_KV):
            q_h = q_ref[:, hh * HEAD_DIM:(hh + 1) * HEAD_DIM]
            s = lax.dot_general(k_c, q_h, _NT, preferred_element_type=_F32)
            s_ref[slot, hh] = s
            mc_ref[slot, hh] = jnp.max(s, axis=0, keepdims=True)

    def step(c, slot, carry, prefetch):
        ms, ls = carry
        if prefetch:
            scores(c + 1, 1 - slot)
        start = pl.multiple_of(c * kv_tile, kv_tile)
        vt_c = vt_ref[:, pl.ds(start, kv_tile)]
        new_ms, new_ls = [], []
        for hh in range(Q_PER_KV):
            m_new = jnp.maximum(ms[hh], mc_ref[slot, hh])
            alpha = jnp.exp2(ms[hh] - m_new)
            p = jnp.exp2(s_ref[slot, hh] - m_new)
            new_ls.append(alpha * ls[hh] + jnp.sum(p, axis=0, keepdims=True))
            new_ms.append(m_new)
            pv = jnp.dot(vt_c, p.astype(_BF16), preferred_element_type=_F32)
            acc_ref[hh] = alpha * acc_ref[hh] + pv
        return tuple(new_ms), tuple(new_ls)

    def pair(jj, carry):
        carry = step(2 * jj, 0, carry, True)
        return step(2 * jj + 1, 1, carry, True)

    scores(0, 0)
    m0 = tuple(jnp.full((1, tq), -1e30, _F32) for _ in range(Q_PER_KV))
    l0 = tuple(jnp.zeros((1, tq), _F32) for _ in range(Q_PER_KV))
    carry = lax.fori_loop(0, n_kv // 2 - 1, pair, (m0, l0))
    carry = step(n_kv - 2, 0, carry, True)
    _, ls = step(n_kv - 1, 1, carry, False)
    for hh in range(Q_PER_KV):
        o_t = acc_ref[hh] / ls[hh]
        o_ref[:, hh * HEAD_DIM:(hh + 1) * HEAD_DIM] = o_t.T.astype(o_ref.dtype)


def _attention(q, k, vt):
    b, s, _ = q.shape
    tq = min(Q_TILE, s)
    tk = min(KV_TILE, s)
    gw = Q_PER_KV * HEAD_DIM
    return pl.pallas_call(
        functools.partial(_attn_kernel, kv_tile=tk),
        grid=(b, N_KV_HEADS, s // tq),
        in_specs=[pl.BlockSpec((None, tq, gw), lambda bi, g, i: (bi, i, g)),
                  pl.BlockSpec((None, s, HEAD_DIM), lambda bi, g, i: (bi, 0, g)),
                  pl.BlockSpec((None, HEAD_DIM, s), lambda bi, g, i: (bi, g, 0))],
        out_specs=pl.BlockSpec((None, tq, gw), lambda bi, g, i: (bi, i, g)),
        out_shape=jax.ShapeDtypeStruct((b, s, ATTN_WIDTH), _BF16),
        scratch_shapes=[pltpu.VMEM((2, Q_PER_KV, tk, tq), _F32),
                        pltpu.VMEM((2, Q_PER_KV, 1, tq), _F32),
                        pltpu.VMEM((Q_PER_KV, HEAD_DIM, tq), _F32)],
        compiler_params=pltpu.CompilerParams(
            dimension_semantics=("parallel", "parallel", "parallel"), vmem_limit_bytes=VMEM_LIMIT),
        name="attn",
    )(q, k, vt)


def _fft_a_kernel(u_ref, cs1_ref, m2_ref, y_ref):
    n1 = u_ref.shape[0]
    gd = FOURIER_GROUP_DIM
    for jj in range(u_ref.shape[1]):
        uj = u_ref[:, jj, :].astype(_BF16)
        pq = jnp.dot(cs1_ref[...], uj, preferred_element_type=_F32)
        for g in range(FOURIER_GROUPS):
            sl = slice(g * gd, (g + 1) * gd)
            lhs = jnp.concatenate([pq[:n1, sl], pq[n1:, sl]], axis=1).astype(_BF16)
            yy = jnp.dot(lhs, m2_ref[...], preferred_element_type=_F32)
            y_ref[:, 0, jj, sl] = yy[:, :gd]
            y_ref[:, 1, jj, sl] = yy[:, gd:]


def _fft_b_kernel(y_ref, g_ref, x_ref):
    for kk in range(y_ref.shape[0]):
        yk = y_ref[kk].astype(_BF16)
        x_ref[:, kk, :] = jnp.dot(g_ref[kk], yk, preferred_element_type=_F32)


def _fourier_mix(u):
    b, s, w = u.shape
    n2 = DFT_RADIX
    n1 = s // n2
    cs1, m2, g = _dft_tables(s)
    ga = min(FFT_GROUP, n2)
    gb = min(FFT_GROUP, n1)
    cparams = pltpu.CompilerParams(dimension_semantics=("parallel", "parallel"),
                                   vmem_limit_bytes=VMEM_LIMIT)
    y = pl.pallas_call(
        _fft_a_kernel,
        grid=(b, n2 // ga),
        in_specs=[pl.BlockSpec((None, n1, ga, w), lambda bi, j: (bi, 0, j, 0)),
                  pl.BlockSpec((2 * n1, n1), lambda bi, j: (0, 0)),
                  pl.BlockSpec((2 * FOURIER_GROUP_DIM, 2 * FOURIER_GROUP_DIM), lambda bi, j: (0, 0))],
        out_specs=pl.BlockSpec((None, n1, 2, ga, w), lambda bi, j: (bi, 0, 0, j, 0)),
        out_shape=jax.ShapeDtypeStruct((b, n1, 2, n2, w), _F32),
        compiler_params=cparams,
        name="fft_a",
    )(u.reshape(b, n1, n2, w), cs1, m2)
    x = pl.pallas_call(
        _fft_b_kernel,
        grid=(b, n1 // gb),
        in_specs=[pl.BlockSpec((None, gb, 2 * n2, w), lambda bi, j: (bi, j, 0, 0)),
                  pl.BlockSpec((gb, n2, 2 * n2), lambda bi, j: (j, 0, 0))],
        out_specs=pl.BlockSpec((None, n2, gb, w), lambda bi, j: (bi, 0, j, 0)),
        out_shape=jax.ShapeDtypeStruct((b, n2, n1, w), _F32),
        compiler_params=cparams,
        name="fft_b",
    )(y.reshape(b, n1, 2 * n2, w), g)
    return x.reshape(b, s, w)


def _merge_kernel(x_ref, o_ref, sa_ref, f_ref, sf_ref, g_ref, wap_ref, wfp_ref, wm_ref, bm_ref,
                  wo_ref, out_ref):
    x = x_ref[...]
    d = x.shape[-1]
    ms = jnp.mean(x * x, axis=-1, keepdims=True)
    h = (x * lax.rsqrt(ms + NORM_EPS) * g_ref[...]).astype(_BF16)
    a = o_ref[...] * sa_ref[...]
    y_attn = jnp.dot(a, wap_ref[...], preferred_element_type=_F32)
    f = (f_ref[...] * sf_ref[...].astype(_F32)).astype(_BF16)
    y_four = jnp.dot(f, wfp_ref[...], preferred_element_type=_F32)
    gates = jax.nn.sigmoid(jnp.dot(h, wm_ref[...], preferred_element_type=_F32) + bm_ref[...])
    merged = (gates[:, :d] * y_attn + gates[:, d:] * y_four).astype(_BF16)
    out_ref[...] = x + jnp.dot(merged, wo_ref[...], preferred_element_type=_F32)


def _merge(x, o, sa, f, sf, norm_g, w_attn_proj, w_fourier_proj, w_merge, b_merge, w_out):
    b, s, d = x.shape
    r = min(ROW_TILE, s)
    row = lambda width: pl.BlockSpec((None, r, width), lambda bi, i: (bi, i, 0))
    full = lambda shape: pl.BlockSpec(shape, lambda bi, i: (0,) * len(shape))
    return pl.pallas_call(
        _merge_kernel,
        grid=(b, s // r),
        in_specs=[row(d), row(ATTN_WIDTH), row(ATTN_WIDTH), row(FOURIER_WIDTH), row(FOURIER_WIDTH),
                  full((1, d)), full((ATTN_WIDTH, d)), full((FOURIER_WIDTH, d)), full((d, 2 * d)),
                  full((1, 2 * d)), full((d, d))],
        out_specs=row(d),
        out_shape=jax.ShapeDtypeStruct((b, s, d), x.dtype),
        compiler_params=pltpu.CompilerParams(
            dimension_semantics=("parallel", "parallel"), vmem_limit_bytes=VMEM_LIMIT),
        name="merge",
    )(x, o, sa, f, sf, norm_g.reshape(1, d), w_attn_proj.astype(_BF16), w_fourier_proj.astype(_BF16),
      w_merge.astype(_BF16), b_merge.reshape(1, 2 * d), w_out.astype(_BF16))


def kernel(x, norm_g, w_in, q_norm_g, k_norm_g, w_attn_proj, w_fourier_proj, w_merge, b_merge, w_out):
    depth = norm_g.shape[0]
    for l in range(depth):
        q, k, vt, sa, u, sf = _proj(x, norm_g[l], w_in[l], q_norm_g[l], k_norm_g[l])
        o = _attention(q, k, vt)
        f = _fourier_mix(u)
        x = _merge(x, o, sa, f, sf, norm_g[l], w_attn_proj[l], w_fourier_proj[l], w_merge[l],
                   b_merge[l], w_out[l])
    return x
```

```python
import functools
import math

import numpy as np
import jax
import jax.numpy as jnp
from jax import lax
from jax.experimental import pallas as pl
from jax.experimental.pallas import tpu as pltpu

HEAD_DIM = 128
N_Q_HEADS = 8
N_KV_HEADS = 2
Q_PER_KV = N_Q_HEADS // N_KV_HEADS
ATTN_WIDTH = N_Q_HEADS * HEAD_DIM
KV_WIDTH = N_KV_HEADS * HEAD_DIM
FOURIER_GROUPS = 4
FOURIER_GROUP_DIM = 128
FOURIER_WIDTH = FOURIER_GROUPS * FOURIER_GROUP_DIM
GRID_W = 64
AXIS_DIM = HEAD_DIM // 2
ROPE_THETA = 10000.0
NORM_EPS = 1e-6

DFT_RADIX = 128
ROW_TILE = 512
Q_TILE = 512
Q_UNIT = 256
KV_TILE = 512
FFT_GROUP = 8
VMEM_LIMIT = 56 * 1024 * 1024

_BF16 = jnp.bfloat16
_F32 = jnp.float32
_NT = (((1,), (1,)), ((), ()))


def _rope_tables(seq_len):
    t = np.arange(seq_len)
    row = (t // GRID_W).astype(np.float32)
    col = (t % GRID_W).astype(np.float32)
    inv = (np.float32(ROPE_THETA) ** (-np.arange(0, AXIS_DIM, 2, dtype=np.float32) / np.float32(AXIS_DIM)))
    inv = inv.astype(np.float32)
    ang_r = (row[:, None] * inv[None, :]).astype(np.float32).astype(np.float64)
    ang_c = (col[:, None] * inv[None, :]).astype(np.float32).astype(np.float64)
    ang = np.concatenate([ang_r, ang_r, ang_c, ang_c], axis=-1)
    sign = np.tile(np.concatenate([-np.ones(AXIS_DIM // 2), np.ones(AXIS_DIM // 2)]), 2)
    return np.cos(ang).astype(np.float32), (np.sin(ang) * sign[None, :]).astype(np.float32)


def _dft_tables(seq_len):
    n1 = seq_len // DFT_RADIX
    n2 = DFT_RADIX
    a = np.arange(n1)
    ang = 2.0 * np.pi * ((a[:, None] * a[None, :]) % n1) / n1
    cs1 = np.concatenate([np.cos(ang), np.sin(ang)], axis=0) / math.sqrt(n1)
    c = np.arange(FOURIER_GROUP_DIM)
    angc = 2.0 * np.pi * ((c[:, None] * c[None, :]) % FOURIER_GROUP_DIM) / FOURIER_GROUP_DIM
    cc, sc = np.cos(angc), np.sin(angc)
    m2 = np.block([[cc, -sc], [-sc, -cc]]) / math.sqrt(FOURIER_GROUP_DIM)
    k1 = np.arange(n1)[:, None, None]
    k2 = np.arange(n2)[None, :, None]
    nn = np.arange(n2)[None, None, :]
    angg = 2.0 * np.pi * ((nn * (k1 + n1 * k2)) % seq_len) / seq_len
    g = np.concatenate([np.cos(angg), np.sin(angg)], axis=-1) / math.sqrt(n2)
    to_bf16 = lambda v: jnp.asarray(v.astype(np.float32)).astype(_BF16)
    return to_bf16(cs1), to_bf16(m2), to_bf16(g)


def _proj_kernel(x_ref, g_ref, w_ref, wvt_ref, qg_ref, kg_ref, cos_ref, sin_ref,
                 q_out, k_out, vt_out, sa_out, u_out, sf_out):
    x = x_ref[...]
    ms = jnp.mean(x * x, axis=-1, keepdims=True)
    h = (x * lax.rsqrt(ms + NORM_EPS) * g_ref[...]).astype(_BF16)

    cos = cos_ref[...]
    sin = sin_ref[...]
    lane = lax.broadcasted_iota(jnp.int32, (1, HEAD_DIM), 1)
    first_half = (lane % AXIS_DIM) < (AXIS_DIM // 2)

    def norm_rope(xh, gain):
        y = xh * lax.rsqrt(jnp.mean(xh * xh, axis=-1, keepdims=True) + NORM_EPS) * gain
        up = pltpu.roll(y, HEAD_DIM - AXIS_DIM // 2, 1)
        down = pltpu.roll(y, AXIS_DIM // 2, 1)
        return y * cos + jnp.where(first_half, up, down) * sin

    q_scale = math.log2(math.e) / math.sqrt(HEAD_DIM)
    o = 0
    pq = jnp.dot(h, w_ref[:, o:o + ATTN_WIDTH], preferred_element_type=_F32)
    for hh in range(N_Q_HEADS):
        sl = slice(hh * HEAD_DIM, (hh + 1) * HEAD_DIM)
        q_out[:, sl] = (norm_rope(pq[:, sl], qg_ref[...]) * q_scale).astype(_BF16)
    o += ATTN_WIDTH
    pk = jnp.dot(h, w_ref[:, o:o + KV_WIDTH], preferred_element_type=_F32)
    for hh in range(N_KV_HEADS):
        sl = slice(hh * HEAD_DIM, (hh + 1) * HEAD_DIM)
        k_out[:, sl] = norm_rope(pk[:, sl], kg_ref[...]).astype(_BF16)
    o += 2 * KV_WIDTH
    vt_out[...] = lax.dot_general(wvt_ref[...], h, _NT, preferred_element_type=_F32).astype(_BF16)
    za = jnp.dot(h, w_ref[:, o:o + ATTN_WIDTH], preferred_element_type=_F32)
    sa_out[...] = (za * jax.nn.sigmoid(za)).astype(_BF16)
    o += ATTN_WIDTH
    u_out[...] = jnp.dot(h, w_ref[:, o:o + FOURIER_WIDTH], preferred_element_type=_F32)
    o += FOURIER_WIDTH
    zf = jnp.dot(h, w_ref[:, o:o + FOURIER_WIDTH], preferred_element_type=_F32)
    sf_out[...] = (zf * jax.nn.sigmoid(zf)).astype(_BF16)


def _proj(x, norm_g, w_in, q_norm_g, k_norm_g):
    b, s, d = x.shape
    r = min(ROW_TILE, s)
    in_w = w_in.shape[1]
    cos, sin = _rope_tables(s)
    w = w_in.astype(_BF16)
    wvt = w_in[:, ATTN_WIDTH + KV_WIDTH:ATTN_WIDTH + 2 * KV_WIDTH].T.astype(_BF16)
    row = lambda width: pl.BlockSpec((None, r, width), lambda bi, i: (bi, i, 0))
    full = lambda shape: pl.BlockSpec(shape, lambda bi, i: (0,) * len(shape))
    tab = pl.BlockSpec((r, HEAD_DIM), lambda bi, i: (i, 0))
    return pl.pallas_call(
        _proj_kernel,
        grid=(b, s // r),
        in_specs=[row(d), full((1, d)), full((d, in_w)), full((KV_WIDTH, d)),
                  full((1, HEAD_DIM)), full((1, HEAD_DIM)), tab, tab],
        out_specs=[row(ATTN_WIDTH), row(KV_WIDTH),
                   pl.BlockSpec((None, KV_WIDTH, r), lambda bi, i: (bi, 0, i)),
                   row(ATTN_WIDTH), row(FOURIER_WIDTH), row(FOURIER_WIDTH)],
        out_shape=[jax.ShapeDtypeStruct((b, s, ATTN_WIDTH), _BF16),
                   jax.ShapeDtypeStruct((b, s, KV_WIDTH), _BF16),
                   jax.ShapeDtypeStruct((b, KV_WIDTH, s), _BF16),
                   jax.ShapeDtypeStruct((b, s, ATTN_WIDTH), _BF16),
                   jax.ShapeDtypeStruct((b, s, FOURIER_WIDTH), _F32),
                   jax.ShapeDtypeStruct((b, s, FOURIER_WIDTH), _BF16)],
        compiler_params=pltpu.CompilerParams(
            dimension_semantics=("parallel", "parallel"), vmem_limit_bytes=VMEM_LIMIT),
        name="proj",
    )(x, norm_g.reshape(1, d), w, wvt, q_norm_g.reshape(1, HEAD_DIM), k_norm_g.reshape(1, HEAD_DIM),
      jnp.asarray(cos), jnp.asarray(sin))


def _attn_kernel(q_ref, k_ref, vt_ref, o_ref, s_ref, mc_ref, acc_ref, *, kv_tile):
    n_units, _, tu = acc_ref.shape
    subs = q_ref.shape[0] // tu
    n_kv = k_ref.shape[0] // kv_tile
    acc_ref[...] = jnp.zeros_like(acc_ref)

    def q_window(u):
        hh, sub = divmod(u, subs)
        return slice(sub * tu, (sub + 1) * tu), slice(hh * HEAD_DIM, (hh + 1) * HEAD_DIM)

    def scores(c, slot, u):
        start = pl.multiple_of(c * kv_tile, kv_tile)
        k_c = k_ref[pl.ds(start, kv_tile), :]
        q_u = q_ref[q_window(u)]
        s = lax.dot_general(k_c, q_u, _NT, preferred_element_type=_F32)
        s_ref[slot, u] = s
        mc_ref[slot, u] = jnp.max(s, axis=0, keepdims=True)

    def step(c, slot, carry, prefetch):
        ms, ls = carry
        start = pl.multiple_of(c * kv_tile, kv_tile)
        vt_c = vt_ref[:, pl.ds(start, kv_tile)]
        new_ms, new_ls = [], []
        for u in range(n_units):
            m_new = jnp.maximum(ms[u], mc_ref[slot, u])
            alpha = jnp.exp2(ms[u] - m_new)
            p = jnp.exp2(s_ref[slot, u] - m_new)
            if prefetch:
                scores(c + 1, 1 - slot, u)
            new_ls.append(alpha * ls[u] + jnp.sum(p, axis=0, keepdims=True))
            new_ms.append(m_new)
            pv = jnp.dot(vt_c, p.astype(_BF16), preferred_element_type=_F32)
            acc_ref[u] = alpha * acc_ref[u] + pv
        return tuple(new_ms), tuple(new_ls)

    def pair(jj, carry):
        carry = step(2 * jj, 0, carry, True)
        return step(2 * jj + 1, 1, carry, True)

    for u in range(n_units):
        scores(0, 0, u)
    m0 = tuple(jnp.full((1, tu), -1e30, _F32) for _ in range(n_units))
    l0 = tuple(jnp.zeros((1, tu), _F32) for _ in range(n_units))
    carry = lax.fori_loop(0, n_kv // 2 - 1, pair, (m0, l0))
    carry = step(n_kv - 2, 0, carry, True)
    _, ls = step(n_kv - 1, 1, carry, False)
    for u in range(n_units):
        o_t = acc_ref[u] / ls[u]
        o_ref[q_window(u)] = o_t.T.astype(o_ref.dtype)


def _attention(q, k, vt):
    b, s, _ = q.shape
    tq = min(Q_TILE, s)
    tk = min(KV_TILE, s)
    gw = Q_PER_KV * HEAD_DIM
    tu = min(Q_UNIT, tq)
    n_units = Q_PER_KV * (tq // tu)
    return pl.pallas_call(
        functools.partial(_attn_kernel, kv_tile=tk),
        grid=(b, N_KV_HEADS, s // tq),
        in_specs=[pl.BlockSpec((None, tq, gw), lambda bi, g, i: (bi, i, g)),
                  pl.BlockSpec((None, s, HEAD_DIM), lambda bi, g, i: (bi, 0, g)),
                  pl.BlockSpec((None, HEAD_DIM, s), lambda bi, g, i: (bi, g, 0))],
        out_specs=pl.BlockSpec((None, tq, gw), lambda bi, g, i: (bi, i, g)),
        out_shape=jax.ShapeDtypeStruct((b, s, ATTN_WIDTH), _BF16),
        scratch_shapes=[pltpu.VMEM((2, n_units, tk, tu), _F32),
                        pltpu.VMEM((2, n_units, 1, tu), _F32),
                        pltpu.VMEM((n_units, HEAD_DIM, tu), _F32)],
        compiler_params=pltpu.CompilerParams(
            dimension_semantics=("parallel", "parallel", "parallel"), vmem_limit_bytes=VMEM_LIMIT),
        name="attn",
    )(q, k, vt)


def _fft_a_kernel(u_ref, cs1_ref, m2_ref, y_ref):
    n1 = u_ref.shape[0]
    gd = FOURIER_GROUP_DIM
    for jj in range(u_ref.shape[1]):
        uj = u_ref[:, jj, :].astype(_BF16)
        pq = jnp.dot(cs1_ref[...], uj, preferred_element_type=_F32)
        for g in range(FOURIER_GROUPS):
            sl = slice(g * gd, (g + 1) * gd)
            lhs = jnp.concatenate([pq[:n1, sl], pq[n1:, sl]], axis=1).astype(_BF16)
            yy = jnp.dot(lhs, m2_ref[...], preferred_element_type=_F32)
            y_ref[:, 0, jj, sl] = yy[:, :gd]
            y_ref[:, 1, jj, sl] = yy[:, gd:]


def _fft_b_kernel(y_ref, g_ref, x_ref):
    for kk in range(y_ref.shape[0]):
        yk = y_ref[kk].astype(_BF16)
        x_ref[:, kk, :] = jnp.dot(g_ref[kk], yk, preferred_element_type=_F32)


def _fourier_mix(u):
    b, s, w = u.shape
    n2 = DFT_RADIX
    n1 = s // n2
    cs1, m2, g = _dft_tables(s)
    ga = min(FFT_GROUP, n2)
    gb = min(FFT_GROUP, n1)
    cparams = pltpu.CompilerParams(dimension_semantics=("parallel", "parallel"),
                                   vmem_limit_bytes=VMEM_LIMIT)
    y = pl.pallas_call(
        _fft_a_kernel,
        grid=(b, n2 // ga),
        in_specs=[pl.BlockSpec((None, n1, ga, w), lambda bi, j: (bi, 0, j, 0)),
                  pl.BlockSpec((2 * n1, n1), lambda bi, j: (0, 0)),
                  pl.BlockSpec((2 * FOURIER_GROUP_DIM, 2 * FOURIER_GROUP_DIM), lambda bi, j: (0, 0))],
        out_specs=pl.BlockSpec((None, n1, 2, ga, w), lambda bi, j: (bi, 0, 0, j, 0)),
        out_shape=jax.ShapeDtypeStruct((b, n1, 2, n2, w), _F32),
        compiler_params=cparams,
        name="fft_a",
    )(u.reshape(b, n1, n2, w), cs1, m2)
    x = pl.pallas_call(
        _fft_b_kernel,
        grid=(b, n1 // gb),
        in_specs=[pl.BlockSpec((None, gb, 2 * n2, w), lambda bi, j: (bi, j, 0, 0)),
                  pl.BlockSpec((gb, n2, 2 * n2), lambda bi, j: (j, 0, 0))],
        out_specs=pl.BlockSpec((None, n2, gb, w), lambda bi, j: (bi, 0, j, 0)),
        out_shape=jax.ShapeDtypeStruct((b, n2, n1, w), _F32),
        compiler_params=cparams,
        name="fft_b",
    )(y.reshape(b, n1, 2 * n2, w), g)
    return x.reshape(b, s, w)


def _merge_kernel(x_ref, o_ref, sa_ref, f_ref, sf_ref, g_ref, wap_ref, wfp_ref, wm_ref, bm_ref,
                  wo_ref, out_ref):
    x = x_ref[...]
    d = x.shape[-1]
    ms = jnp.mean(x * x, axis=-1, keepdims=True)
    h = (x * lax.rsqrt(ms + NORM_EPS) * g_ref[...]).astype(_BF16)
    a = o_ref[...] * sa_ref[...]
    y_attn = jnp.dot(a, wap_ref[...], preferred_element_type=_F32)
    f = (f_ref[...] * sf_ref[...].astype(_F32)).astype(_BF16)
    y_four = jnp.dot(f, wfp_ref[...], preferred_element_type=_F32)
    gates = jax.nn.sigmoid(jnp.dot(h, wm_ref[...], preferred_element_type=_F32) + bm_ref[...])
    merged = (gates[:, :d] * y_attn + gates[:, d:] * y_four).astype(_BF16)
    out_ref[...] = x + jnp.dot(merged, wo_ref[...], preferred_element_type=_F32)


def _merge(x, o, sa, f, sf, norm_g, w_attn_proj, w_fourier_proj, w_merge, b_merge, w_out):
    b, s, d = x.shape
    r = min(ROW_TILE, s)
    row = lambda width: pl.BlockSpec((None, r, width), lambda bi, i: (bi, i, 0))
    full = lambda shape: pl.BlockSpec(shape, lambda bi, i: (0,) * len(shape))
    return pl.pallas_call(
        _merge_kernel,
        grid=(b, s // r),
        in_specs=[row(d), row(ATTN_WIDTH), row(ATTN_WIDTH), row(FOURIER_WIDTH), row(FOURIER_WIDTH),
                  full((1, d)), full((ATTN_WIDTH, d)), full((FOURIER_WIDTH, d)), full((d, 2 * d)),
                  full((1, 2 * d)), full((d, d))],
        out_specs=row(d),
        out_shape=jax.ShapeDtypeStruct((b, s, d), x.dtype),
        compiler_params=pltpu.CompilerParams(
            dimension_semantics=("parallel", "parallel"), vmem_limit_bytes=VMEM_LIMIT),
        name="merge",
    )(x, o, sa, f, sf, norm_g.reshape(1, d), w_attn_proj.astype(_BF16), w_fourier_proj.astype(_BF16),
      w_merge.astype(_BF16), b_merge.reshape(1, 2 * d), w_out.astype(_BF16))


def kernel(x, norm_g, w_in, q_norm_g, k_norm_g, w_attn_proj, w_fourier_proj, w_merge, b_merge, w_out):
    depth = norm_g.shape[0]
    for l in range(depth):
        q, k, vt, sa, u, sf = _proj(x, norm_g[l], w_in[l], q_norm_g[l], k_norm_g[l])
        o = _attention(q, k, vt)
        f = _fourier_mix(u)
        x = _merge(x, o, sa, f, sf, norm_g[l], w_attn_proj[l], w_fourier_proj[l], w_merge[l],
                   b_merge[l], w_out[l])
    return x
```

```python
import functools
import math

import numpy as np
import jax
import jax.numpy as jnp
from jax import lax
from jax.experimental import pallas as pl
from jax.experimental.pallas import tpu as pltpu

HEAD_DIM = 128
N_Q_HEADS = 8
N_KV_HEADS = 2
Q_PER_KV = N_Q_HEADS // N_KV_HEADS
ATTN_WIDTH = N_Q_HEADS * HEAD_DIM
KV_WIDTH = N_KV_HEADS * HEAD_DIM
FOURIER_GROUPS = 4
FOURIER_GROUP_DIM = 128
FOURIER_WIDTH = FOURIER_GROUPS * FOURIER_GROUP_DIM
GRID_W = 64
AXIS_DIM = HEAD_DIM // 2
ROPE_THETA = 10000.0
NORM_EPS = 1e-6

DFT_RADIX = 128
ROW_TILE = 512
Q_TILE = 512
Q_UNIT = 256
KV_TILE = 512
FFT_GROUP = 8
ONES_ROWS = 16
VMEM_LIMIT = 56 * 1024 * 1024

_BF16 = jnp.bfloat16
_F32 = jnp.float32
_NT = (((1,), (1,)), ((), ()))


def _rope_tables(seq_len):
    t = np.arange(seq_len)
    row = (t // GRID_W).astype(np.float32)
    col = (t % GRID_W).astype(np.float32)
    inv = (np.float32(ROPE_THETA) ** (-np.arange(0, AXIS_DIM, 2, dtype=np.float32) / np.float32(AXIS_DIM)))
    inv = inv.astype(np.float32)
    ang_r = (row[:, None] * inv[None, :]).astype(np.float32).astype(np.float64)
    ang_c = (col[:, None] * inv[None, :]).astype(np.float32).astype(np.float64)
    ang = np.concatenate([ang_r, ang_r, ang_c, ang_c], axis=-1)
    sign = np.tile(np.concatenate([-np.ones(AXIS_DIM // 2), np.ones(AXIS_DIM // 2)]), 2)
    return np.cos(ang).astype(np.float32), (np.sin(ang) * sign[None, :]).astype(np.float32)


def _dft_tables(seq_len):
    n1 = seq_len // DFT_RADIX
    n2 = DFT_RADIX
    a = np.arange(n1)
    ang = 2.0 * np.pi * ((a[:, None] * a[None, :]) % n1) / n1
    cs1 = np.concatenate([np.cos(ang), np.sin(ang)], axis=0) / math.sqrt(n1)
    c = np.arange(FOURIER_GROUP_DIM)
    angc = 2.0 * np.pi * ((c[:, None] * c[None, :]) % FOURIER_GROUP_DIM) / FOURIER_GROUP_DIM
    cc, sc = np.cos(angc), np.sin(angc)
    m2 = np.block([[cc, -sc], [-sc, -cc]]) / math.sqrt(FOURIER_GROUP_DIM)
    k1 = np.arange(n1)[:, None, None]
    k2 = np.arange(n2)[None, :, None]
    nn = np.arange(n2)[None, None, :]
    angg = 2.0 * np.pi * ((nn * (k1 + n1 * k2)) % seq_len) / seq_len
    g = np.concatenate([np.cos(angg), np.sin(angg)], axis=-1) / math.sqrt(n2)
    to_bf16 = lambda v: jnp.asarray(v.astype(np.float32)).astype(_BF16)
    return to_bf16(cs1), to_bf16(m2), to_bf16(g)


def _proj_kernel(x_ref, g_ref, w_ref, wvt_ref, qg_ref, kg_ref, cos_ref, sin_ref,
                 q_out, k_out, vt_out, sa_out, u_out, sf_out):
    x = x_ref[...]
    ms = jnp.mean(x * x, axis=-1, keepdims=True)
    h = (x * lax.rsqrt(ms + NORM_EPS) * g_ref[...]).astype(_BF16)

    cos = cos_ref[...]
    sin = sin_ref[...]
    lane = lax.broadcasted_iota(jnp.int32, (1, HEAD_DIM), 1)
    first_half = (lane % AXIS_DIM) < (AXIS_DIM // 2)

    def norm_rope(xh, gain):
        y = xh * lax.rsqrt(jnp.mean(xh * xh, axis=-1, keepdims=True) + NORM_EPS) * gain
        up = pltpu.roll(y, HEAD_DIM - AXIS_DIM // 2, 1)
        down = pltpu.roll(y, AXIS_DIM // 2, 1)
        return y * cos + jnp.where(first_half, up, down) * sin

    q_scale = math.log2(math.e) / math.sqrt(HEAD_DIM)
    o = 0
    pq = jnp.dot(h, w_ref[:, o:o + ATTN_WIDTH], preferred_element_type=_F32)
    for hh in range(N_Q_HEADS):
        sl = slice(hh * HEAD_DIM, (hh + 1) * HEAD_DIM)
        q_out[:, sl] = (norm_rope(pq[:, sl], qg_ref[...]) * q_scale).astype(_BF16)
    o += ATTN_WIDTH
    pk = jnp.dot(h, w_ref[:, o:o + KV_WIDTH], preferred_element_type=_F32)
    for hh in range(N_KV_HEADS):
        sl = slice(hh * HEAD_DIM, (hh + 1) * HEAD_DIM)
        k_out[:, sl] = norm_rope(pk[:, sl], kg_ref[...]).astype(_BF16)
    o += 2 * KV_WIDTH
    vt_out[...] = lax.dot_general(wvt_ref[...], h, _NT, preferred_element_type=_F32).astype(_BF16)
    za = jnp.dot(h, w_ref[:, o:o + ATTN_WIDTH], preferred_element_type=_F32)
    sa_out[...] = (za * jax.nn.sigmoid(za)).astype(_BF16)
    o += ATTN_WIDTH
    u_out[...] = jnp.dot(h, w_ref[:, o:o + FOURIER_WIDTH], preferred_element_type=_F32)
    o += FOURIER_WIDTH
    zf = jnp.dot(h, w_ref[:, o:o + FOURIER_WIDTH], preferred_element_type=_F32)
    sf_out[...] = (zf * jax.nn.sigmoid(zf)).astype(_BF16)


def _proj(x, norm_g, w_in, q_norm_g, k_norm_g):
    b, s, d = x.shape
    r = min(ROW_TILE, s)
    in_w = w_in.shape[1]
    cos, sin = _rope_tables(s)
    w = w_in.astype(_BF16)
    wvt = w_in[:, ATTN_WIDTH + KV_WIDTH:ATTN_WIDTH + 2 * KV_WIDTH].T.astype(_BF16)
    row = lambda width: pl.BlockSpec((None, r, width), lambda bi, i: (bi, i, 0))
    full = lambda shape: pl.BlockSpec(shape, lambda bi, i: (0,) * len(shape))
    tab = pl.BlockSpec((r, HEAD_DIM), lambda bi, i: (i, 0))
    return pl.pallas_call(
        _proj_kernel,
        grid=(b, s // r),
        in_specs=[row(d), full((1, d)), full((d, in_w)), full((KV_WIDTH, d)),
                  full((1, HEAD_DIM)), full((1, HEAD_DIM)), tab, tab],
        out_specs=[row(ATTN_WIDTH), row(KV_WIDTH),
                   pl.BlockSpec((None, KV_WIDTH, r), lambda bi, i: (bi, 0, i)),
                   row(ATTN_WIDTH), row(FOURIER_WIDTH), row(FOURIER_WIDTH)],
        out_shape=[jax.ShapeDtypeStruct((b, s, ATTN_WIDTH), _BF16),
                   jax.ShapeDtypeStruct((b, s, KV_WIDTH), _BF16),
                   jax.ShapeDtypeStruct((b, KV_WIDTH, s), _BF16),
                   jax.ShapeDtypeStruct((b, s, ATTN_WIDTH), _BF16),
                   jax.ShapeDtypeStruct((b, s, FOURIER_WIDTH), _F32),
                   jax.ShapeDtypeStruct((b, s, FOURIER_WIDTH), _BF16)],
        compiler_params=pltpu.CompilerParams(
            dimension_semantics=("parallel", "parallel"), vmem_limit_bytes=VMEM_LIMIT),
        name="proj",
    )(x, norm_g.reshape(1, d), w, wvt, q_norm_g.reshape(1, HEAD_DIM), k_norm_g.reshape(1, HEAD_DIM),
      jnp.asarray(cos), jnp.asarray(sin))


def _attn_kernel(q_ref, k_ref, vt_ref, o_ref, s_ref, mc_ref, acc_ref, *, kv_tile):
    n_units, _, tu = acc_ref.shape
    subs = q_ref.shape[0] // tu
    n_kv = k_ref.shape[0] // kv_tile
    acc_ref[...] = jnp.zeros_like(acc_ref)

    def q_window(u):
        hh, sub = divmod(u, subs)
        return slice(sub * tu, (sub + 1) * tu), slice(hh * HEAD_DIM, (hh + 1) * HEAD_DIM)

    def scores(c, slot, u):
        start = pl.multiple_of(c * kv_tile, kv_tile)
        k_c = k_ref[pl.ds(start, kv_tile), :]
        q_u = q_ref[q_window(u)]
        s = lax.dot_general(k_c, q_u, _NT, preferred_element_type=_F32)
        s_ref[slot, u] = s
        mc_ref[slot, u] = jnp.max(s, axis=0, keepdims=True)

    def step(c, slot, carry, prefetch):
        ms, ls = carry
        start = pl.multiple_of(c * kv_tile, kv_tile)
        vt_c = jnp.concatenate([vt_ref[:, pl.ds(start, kv_tile)],
                                jnp.ones((ONES_ROWS, kv_tile), _BF16)], axis=0)
        new_ms, new_ls = [], []
        for u in range(n_units):
            m_new = jnp.maximum(ms[u], mc_ref[slot, u])
            alpha = jnp.exp2(ms[u] - m_new)
            p = jnp.exp2(s_ref[slot, u] - m_new)
            if prefetch:
                scores(c + 1, 1 - slot, u)
            new_ms.append(m_new)
            pv = jnp.dot(vt_c, p.astype(_BF16), preferred_element_type=_F32)
            new_ls.append(alpha * ls[u] + pv[HEAD_DIM:HEAD_DIM + 1])
            acc_ref[u] = alpha * acc_ref[u] + pv[:HEAD_DIM]
        return tuple(new_ms), tuple(new_ls)

    def quad(jj, carry):
        carry = step(4 * jj, 0, carry, True)
        carry = step(4 * jj + 1, 1, carry, True)
        carry = step(4 * jj + 2, 0, carry, True)
        return step(4 * jj + 3, 1, carry, True)

    for u in range(n_units):
        scores(0, 0, u)
    m0 = tuple(jnp.full((1, tu), -1e30, _F32) for _ in range(n_units))
    l0 = tuple(jnp.zeros((1, tu), _F32) for _ in range(n_units))
    carry = lax.fori_loop(0, n_kv // 4 - 1, quad, (m0, l0))
    carry = step(n_kv - 4, 0, carry, True)
    carry = step(n_kv - 3, 1, carry, True)
    carry = step(n_kv - 2, 0, carry, True)
    _, ls = step(n_kv - 1, 1, carry, False)
    for u in range(n_units):
        o_t = acc_ref[u] / ls[u]
        o_ref[q_window(u)] = o_t.T.astype(o_ref.dtype)


def _attention(q, k, vt):
    b, s, _ = q.shape
    tq = min(Q_TILE, s)
    tk = min(KV_TILE, s)
    gw = Q_PER_KV * HEAD_DIM
    tu = min(Q_UNIT, tq)
    n_units = Q_PER_KV * (tq // tu)
    return pl.pallas_call(
        functools.partial(_attn_kernel, kv_tile=tk),
        grid=(b, N_KV_HEADS, s // tq),
        in_specs=[pl.BlockSpec((None, tq, gw), lambda bi, g, i: (bi, i, g)),
                  pl.BlockSpec((None, s, HEAD_DIM), lambda bi, g, i: (bi, 0, g)),
                  pl.BlockSpec((None, HEAD_DIM, s), lambda bi, g, i: (bi, g, 0))],
        out_specs=pl.BlockSpec((None, tq, gw), lambda bi, g, i: (bi, i, g)),
        out_shape=jax.ShapeDtypeStruct((b, s, ATTN_WIDTH), _BF16),
        scratch_shapes=[pltpu.VMEM((2, n_units, tk, tu), _F32),
                        pltpu.VMEM((2, n_units, 1, tu), _F32),
                        pltpu.VMEM((n_units, HEAD_DIM, tu), _F32)],
        compiler_params=pltpu.CompilerParams(
            dimension_semantics=("parallel", "parallel", "parallel"), vmem_limit_bytes=VMEM_LIMIT),
        name="attn",
    )(q, k, vt)


def _fft_a_kernel(u_ref, cs1_ref, m2_ref, y_ref):
    n1 = u_ref.shape[0]
    gd = FOURIER_GROUP_DIM
    for jj in range(u_ref.shape[1]):
        uj = u_ref[:, jj, :].astype(_BF16)
        pq = jnp.dot(cs1_ref[...], uj, preferred_element_type=_F32)
        for g in range(FOURIER_GROUPS):
            sl = slice(g * gd, (g + 1) * gd)
            lhs = jnp.concatenate([pq[:n1, sl], pq[n1:, sl]], axis=1).astype(_BF16)
            yy = jnp.dot(lhs, m2_ref[...], preferred_element_type=_F32)
            y_ref[:, 0, jj, sl] = yy[:, :gd]
            y_ref[:, 1, jj, sl] = yy[:, gd:]


def _fft_b_kernel(y_ref, g_ref, x_ref):
    for kk in range(y_ref.shape[0]):
        yk = y_ref[kk].astype(_BF16)
        x_ref[:, kk, :] = jnp.dot(g_ref[kk], yk, preferred_element_type=_F32)


def _fourier_mix(u):
    b, s, w = u.shape
    n2 = DFT_RADIX
    n1 = s // n2
    cs1, m2, g = _dft_tables(s)
    ga = min(FFT_GROUP, n2)
    gb = min(FFT_GROUP, n1)
    cparams = pltpu.CompilerParams(dimension_semantics=("parallel", "parallel"),
                                   vmem_limit_bytes=VMEM_LIMIT)
    y = pl.pallas_call(
        _fft_a_kernel,
        grid=(b, n2 // ga),
        in_specs=[pl.BlockSpec((None, n1, ga, w), lambda bi, j: (bi, 0, j, 0)),
                  pl.BlockSpec((2 * n1, n1), lambda bi, j: (0, 0)),
                  pl.BlockSpec((2 * FOURIER_GROUP_DIM, 2 * FOURIER_GROUP_DIM), lambda bi, j: (0, 0))],
        out_specs=pl.BlockSpec((None, n1, 2, ga, w), lambda bi, j: (bi, 0, 0, j, 0)),
        out_shape=jax.ShapeDtypeStruct((b, n1, 2, n2, w), _F32),
        compiler_params=cparams,
        name="fft_a",
    )(u.reshape(b, n1, n2, w), cs1, m2)
    x = pl.pallas_call(
        _fft_b_kernel,
        grid=(b, n1 // gb),
        in_specs=[pl.BlockSpec((None, gb, 2 * n2, w), lambda bi, j: (bi, j, 0, 0)),
                  pl.BlockSpec((gb, n2, 2 * n2), lambda bi, j: (j, 0, 0))],
        out_specs=pl.BlockSpec((None, n2, gb, w), lambda bi, j: (bi, 0, j, 0)),
        out_shape=jax.ShapeDtypeStruct((b, n2, n1, w), _F32),
        compiler_params=cparams,
        name="fft_b",
    )(y.reshape(b, n1, 2 * n2, w), g)
    return x.reshape(b, s, w)


def _merge_kernel(x_ref, o_ref, sa_ref, f_ref, sf_ref, g_ref, wap_ref, wfp_ref, wm_ref, bm_ref,
                  wo_ref, out_ref):
    x = x_ref[...]
    d = x.shape[-1]
    ms = jnp.mean(x * x, axis=-1, keepdims=True)
    h = (x * lax.rsqrt(ms + NORM_EPS) * g_ref[...]).astype(_BF16)
    a = o_ref[...] * sa_ref[...]
    y_attn = jnp.dot(a, wap_ref[...], preferred_element_type=_F32)
    f = (f_ref[...] * sf_ref[...].astype(_F32)).astype(_BF16)
    y_four = jnp.dot(f, wfp_ref[...], preferred_element_type=_F32)
    gates = jax.nn.sigmoid(jnp.dot(h, wm_ref[...], preferred_element_type=_F32) + bm_ref[...])
    merged = (gates[:, :d] * y_attn + gates[:, d:] * y_four).astype(_BF16)
    out_ref[...] = x + jnp.dot(merged, wo_ref[...], preferred_element_type=_F32)


def _merge(x, o, sa, f, sf, norm_g, w_attn_proj, w_fourier_proj, w_merge, b_merge, w_out):
    b, s, d = x.shape
    r = min(ROW_TILE, s)
    row = lambda width: pl.BlockSpec((None, r, width), lambda bi, i: (bi, i, 0))
    full = lambda shape: pl.BlockSpec(shape, lambda bi, i: (0,) * len(shape))
    return pl.pallas_call(
        _merge_kernel,
        grid=(b, s // r),
        in_specs=[row(d), row(ATTN_WIDTH), row(ATTN_WIDTH), row(FOURIER_WIDTH), row(FOURIER_WIDTH),
                  full((1, d)), full((ATTN_WIDTH, d)), full((FOURIER_WIDTH, d)), full((d, 2 * d)),
                  full((1, 2 * d)), full((d, d))],
        out_specs=row(d),
        out_shape=jax.ShapeDtypeStruct((b, s, d), x.dtype),
        compiler_params=pltpu.CompilerParams(
            dimension_semantics=("parallel", "parallel"), vmem_limit_bytes=VMEM_LIMIT),
        name="merge",
    )(x, o, sa, f, sf, norm_g.reshape(1, d), w_attn_proj.astype(_BF16), w_fourier_proj.astype(_BF16),
      w_merge.astype(_BF16), b_merge.reshape(1, 2 * d), w_out.astype(_BF16))


def kernel(x, norm_g, w_in, q_norm_g, k_norm_g, w_attn_proj, w_fourier_proj, w_merge, b_merge, w_out):
    depth = norm_g.shape[0]
    for l in range(depth):
        q, k, vt, sa, u, sf = _proj(x, norm_g[l], w_in[l], q_norm_g[l], k_norm_g[l])
        o = _attention(q, k, vt)
        f = _fourier_mix(u)
        x = _merge(x, o, sa, f, sf, norm_g[l], w_attn_proj[l], w_fourier_proj[l], w_merge[l],
                   b_merge[l], w_out[l])
    return x
```

```python
import functools
import math

import numpy as np
import jax
import jax.numpy as jnp
from jax import lax
from jax.experimental import pallas as pl
from jax.experimental.pallas import tpu as pltpu

HEAD_DIM = 128
N_Q_HEADS = 8
N_KV_HEADS = 2
Q_PER_KV = N_Q_HEADS // N_KV_HEADS
ATTN_WIDTH = N_Q_HEADS * HEAD_DIM
KV_WIDTH = N_KV_HEADS * HEAD_DIM
FOURIER_GROUPS = 4
FOURIER_GROUP_DIM = 128
FOURIER_WIDTH = FOURIER_GROUPS * FOURIER_GROUP_DIM
GRID_W = 64
AXIS_DIM = HEAD_DIM // 2
ROPE_THETA = 10000.0
NORM_EPS = 1e-6

DFT_RADIX = 128
ROW_TILE = 512
Q_TILE = 512
Q_UNIT = 256
KV_TILE = 512
CHUNKS_PER_TRIP = 8
FFT_GROUP = 8
ONES_ROWS = 16
VMEM_LIMIT = 56 * 1024 * 1024

_BF16 = jnp.bfloat16
_F32 = jnp.float32
_NT = (((1,), (1,)), ((), ()))


def _rope_tables(seq_len):
    t = np.arange(seq_len)
    row = (t // GRID_W).astype(np.float32)
    col = (t % GRID_W).astype(np.float32)
    inv = (np.float32(ROPE_THETA) ** (-np.arange(0, AXIS_DIM, 2, dtype=np.float32) / np.float32(AXIS_DIM)))
    inv = inv.astype(np.float32)
    ang_r = (row[:, None] * inv[None, :]).astype(np.float32).astype(np.float64)
    ang_c = (col[:, None] * inv[None, :]).astype(np.float32).astype(np.float64)
    ang = np.concatenate([ang_r, ang_r, ang_c, ang_c], axis=-1)
    sign = np.tile(np.concatenate([-np.ones(AXIS_DIM // 2), np.ones(AXIS_DIM // 2)]), 2)
    return np.cos(ang).astype(np.float32), (np.sin(ang) * sign[None, :]).astype(np.float32)


def _dft_tables(seq_len):
    n1 = seq_len // DFT_RADIX
    n2 = DFT_RADIX
    a = np.arange(n1)
    ang = 2.0 * np.pi * ((a[:, None] * a[None, :]) % n1) / n1
    cs1 = np.concatenate([np.cos(ang), np.sin(ang)], axis=0) / math.sqrt(n1)
    c = np.arange(FOURIER_GROUP_DIM)
    angc = 2.0 * np.pi * ((c[:, None] * c[None, :]) % FOURIER_GROUP_DIM) / FOURIER_GROUP_DIM
    cc, sc = np.cos(angc), np.sin(angc)
    m2 = np.block([[cc, -sc], [-sc, -cc]]) / math.sqrt(FOURIER_GROUP_DIM)
    k1 = np.arange(n1)[:, None, None]
    k2 = np.arange(n2)[None, :, None]
    nn = np.arange(n2)[None, None, :]
    angg = 2.0 * np.pi * ((nn * (k1 + n1 * k2)) % seq_len) / seq_len
    g = np.concatenate([np.cos(angg), np.sin(angg)], axis=-1) / math.sqrt(n2)
    to_bf16 = lambda v: jnp.asarray(v.astype(np.float32)).astype(_BF16)
    return to_bf16(cs1), to_bf16(m2), to_bf16(g)


def _proj_kernel(x_ref, g_ref, w_ref, wvt_ref, qg_ref, kg_ref, cos_ref, sin_ref,
                 q_out, k_out, vt_out, sa_out, u_out, sf_out):
    x = x_ref[...]
    ms = jnp.mean(x * x, axis=-1, keepdims=True)
    h = (x * lax.rsqrt(ms + NORM_EPS) * g_ref[...]).astype(_BF16)

    cos = cos_ref[...]
    sin = sin_ref[...]
    lane = lax.broadcasted_iota(jnp.int32, (1, HEAD_DIM), 1)
    first_half = (lane % AXIS_DIM) < (AXIS_DIM // 2)

    def norm_rope(xh, gain):
        y = xh * lax.rsqrt(jnp.mean(xh * xh, axis=-1, keepdims=True) + NORM_EPS) * gain
        up = pltpu.roll(y, HEAD_DIM - AXIS_DIM // 2, 1)
        down = pltpu.roll(y, AXIS_DIM // 2, 1)
        return y * cos + jnp.where(first_half, up, down) * sin

    q_scale = math.log2(math.e) / math.sqrt(HEAD_DIM)
    o = 0
    pq = jnp.dot(h, w_ref[:, o:o + ATTN_WIDTH], preferred_element_type=_F32)
    for hh in range(N_Q_HEADS):
        sl = slice(hh * HEAD_DIM, (hh + 1) * HEAD_DIM)
        q_out[:, sl] = (norm_rope(pq[:, sl], qg_ref[...]) * q_scale).astype(_BF16)
    o += ATTN_WIDTH
    pk = jnp.dot(h, w_ref[:, o:o + KV_WIDTH], preferred_element_type=_F32)
    for hh in range(N_KV_HEADS):
        sl = slice(hh * HEAD_DIM, (hh + 1) * HEAD_DIM)
        k_out[:, sl] = norm_rope(pk[:, sl], kg_ref[...]).astype(_BF16)
    o += 2 * KV_WIDTH
    vt_out[...] = lax.dot_general(wvt_ref[...], h, _NT, preferred_element_type=_F32).astype(_BF16)
    za = jnp.dot(h, w_ref[:, o:o + ATTN_WIDTH], preferred_element_type=_F32)
    sa_out[...] = (za * jax.nn.sigmoid(za)).astype(_BF16)
    o += ATTN_WIDTH
    u_out[...] = jnp.dot(h, w_ref[:, o:o + FOURIER_WIDTH], preferred_element_type=_F32)
    o += FOURIER_WIDTH
    zf = jnp.dot(h, w_ref[:, o:o + FOURIER_WIDTH], preferred_element_type=_F32)
    sf_out[...] = (zf * jax.nn.sigmoid(zf)).astype(_BF16)


def _proj(x, norm_g, w_in, q_norm_g, k_norm_g):
    b, s, d = x.shape
    r = min(ROW_TILE, s)
    in_w = w_in.shape[1]
    cos, sin = _rope_tables(s)
    w = w_in.astype(_BF16)
    wvt = w_in[:, ATTN_WIDTH + KV_WIDTH:ATTN_WIDTH + 2 * KV_WIDTH].T.astype(_BF16)
    row = lambda width: pl.BlockSpec((None, r, width), lambda bi, i: (bi, i, 0))
    full = lambda shape: pl.BlockSpec(shape, lambda bi, i: (0,) * len(shape))
    tab = pl.BlockSpec((r, HEAD_DIM), lambda bi, i: (i, 0))
    return pl.pallas_call(
        _proj_kernel,
        grid=(b, s // r),
        in_specs=[row(d), full((1, d)), full((d, in_w)), full((KV_WIDTH, d)),
                  full((1, HEAD_DIM)), full((1, HEAD_DIM)), tab, tab],
        out_specs=[row(ATTN_WIDTH), row(KV_WIDTH),
                   pl.BlockSpec((None, KV_WIDTH, r), lambda bi, i: (bi, 0, i)),
                   row(ATTN_WIDTH), row(FOURIER_WIDTH), row(FOURIER_WIDTH)],
        out_shape=[jax.ShapeDtypeStruct((b, s, ATTN_WIDTH), _BF16),
                   jax.ShapeDtypeStruct((b, s, KV_WIDTH), _BF16),
                   jax.ShapeDtypeStruct((b, KV_WIDTH, s), _BF16),
                   jax.ShapeDtypeStruct((b, s, ATTN_WIDTH), _BF16),
                   jax.ShapeDtypeStruct((b, s, FOURIER_WIDTH), _F32),
                   jax.ShapeDtypeStruct((b, s, FOURIER_WIDTH), _BF16)],
        compiler_params=pltpu.CompilerParams(
            dimension_semantics=("parallel", "parallel"), vmem_limit_bytes=VMEM_LIMIT),
        name="proj",
    )(x, norm_g.reshape(1, d), w, wvt, q_norm_g.reshape(1, HEAD_DIM), k_norm_g.reshape(1, HEAD_DIM),
      jnp.asarray(cos), jnp.asarray(sin))


def _attn_kernel(q_ref, k_ref, vt_ref, o_ref, s_ref, mc_ref, acc_ref, *, kv_tile):
    n_units, _, tu = acc_ref.shape
    subs = q_ref.shape[0] // tu
    n_kv = k_ref.shape[0] // kv_tile
    acc_ref[...] = jnp.zeros_like(acc_ref)

    def q_window(u):
        hh, sub = divmod(u, subs)
        return slice(sub * tu, (sub + 1) * tu), slice(hh * HEAD_DIM, (hh + 1) * HEAD_DIM)

    def scores(c, slot, u):
        start = pl.multiple_of(c * kv_tile, kv_tile)
        k_c = k_ref[pl.ds(start, kv_tile), :]
        q_u = q_ref[q_window(u)]
        s = lax.dot_general(k_c, q_u, _NT, preferred_element_type=_F32)
        s_ref[slot, u] = s
        mc_ref[slot, u] = jnp.max(s, axis=0, keepdims=True)

    def step(c, slot, carry, prefetch):
        ms, ls = carry
        start = pl.multiple_of(c * kv_tile, kv_tile)
        vt_c = jnp.concatenate([vt_ref[:, pl.ds(start, kv_tile)],
                                jnp.ones((ONES_ROWS, kv_tile), _BF16)], axis=0)
        new_ms, new_ls = [], []
        for u in range(n_units):
            m_new = jnp.maximum(ms[u], mc_ref[slot, u])
            alpha = jnp.exp2(ms[u] - m_new)
            p = jnp.exp2(s_ref[slot, u] - m_new)
            if prefetch:
                scores(c + 1, 1 - slot, u)
            new_ms.append(m_new)
            pv = jnp.dot(vt_c, p.astype(_BF16), preferred_element_type=_F32)
            new_ls.append(alpha * ls[u] + pv[HEAD_DIM:HEAD_DIM + 1])
            acc_ref[u] = alpha * acc_ref[u] + pv[:HEAD_DIM]
        return tuple(new_ms), tuple(new_ls)

    def octet(jj, carry):
        for i in range(CHUNKS_PER_TRIP - 1):
            carry = step(CHUNKS_PER_TRIP * jj + i, i % 2, carry, True)
        return step(CHUNKS_PER_TRIP * jj + CHUNKS_PER_TRIP - 1, 1, carry, True)

    for u in range(n_units):
        scores(0, 0, u)
    m0 = tuple(jnp.full((1, tu), -1e30, _F32) for _ in range(n_units))
    l0 = tuple(jnp.zeros((1, tu), _F32) for _ in range(n_units))
    carry = lax.fori_loop(0, n_kv // CHUNKS_PER_TRIP - 1, octet, (m0, l0))
    for i in range(CHUNKS_PER_TRIP, 1, -1):
        carry = step(n_kv - i, i % 2, carry, True)
    _, ls = step(n_kv - 1, 1, carry, False)
    for u in range(n_units):
        o_t = acc_ref[u] / ls[u]
        o_ref[q_window(u)] = o_t.T.astype(o_ref.dtype)


def _attention(q, k, vt):
    b, s, _ = q.shape
    tq = min(Q_TILE, s)
    tk = min(KV_TILE, s)
    gw = Q_PER_KV * HEAD_DIM
    tu = min(Q_UNIT, tq)
    n_units = Q_PER_KV * (tq // tu)
    return pl.pallas_call(
        functools.partial(_attn_kernel, kv_tile=tk),
        grid=(b, N_KV_HEADS, s // tq),
        in_specs=[pl.BlockSpec((None, tq, gw), lambda bi, g, i: (bi, i, g)),
                  pl.BlockSpec((None, s, HEAD_DIM), lambda bi, g, i: (bi, 0, g)),
                  pl.BlockSpec((None, HEAD_DIM, s), lambda bi, g, i: (bi, g, 0))],
        out_specs=pl.BlockSpec((None, tq, gw), lambda bi, g, i: (bi, i, g)),
        out_shape=jax.ShapeDtypeStruct((b, s, ATTN_WIDTH), _BF16),
        scratch_shapes=[pltpu.VMEM((2, n_units, tk, tu), _F32),
                        pltpu.VMEM((2, n_units, 1, tu), _F32),
                        pltpu.VMEM((n_units, HEAD_DIM, tu), _F32)],
        compiler_params=pltpu.CompilerParams(
            dimension_semantics=("parallel", "parallel", "parallel"), vmem_limit_bytes=VMEM_LIMIT),
        name="attn",
    )(q, k, vt)


def _fft_a_kernel(u_ref, cs1_ref, m2_ref, y_ref):
    n1 = u_ref.shape[0]
    gd = FOURIER_GROUP_DIM
    for jj in range(u_ref.shape[1]):
        uj = u_ref[:, jj, :].astype(_BF16)
        pq = jnp.dot(cs1_ref[...], uj, preferred_element_type=_F32)
        for g in range(FOURIER_GROUPS):
            sl = slice(g * gd, (g + 1) * gd)
            lhs = jnp.concatenate([pq[:n1, sl], pq[n1:, sl]], axis=1).astype(_BF16)
            yy = jnp.dot(lhs, m2_ref[...], preferred_element_type=_F32)
            y_ref[:, 0, jj, sl] = yy[:, :gd]
            y_ref[:, 1, jj, sl] = yy[:, gd:]


def _fft_b_kernel(y_ref, g_ref, x_ref):
    for kk in range(y_ref.shape[0]):
        yk = y_ref[kk].astype(_BF16)
        x_ref[:, kk, :] = jnp.dot(g_ref[kk], yk, preferred_element_type=_F32)


def _fourier_mix(u):
    b, s, w = u.shape
    n2 = DFT_RADIX
    n1 = s // n2
    cs1, m2, g = _dft_tables(s)
    ga = min(FFT_GROUP, n2)
    gb = min(FFT_GROUP, n1)
    cparams = pltpu.CompilerParams(dimension_semantics=("parallel", "parallel"),
                                   vmem_limit_bytes=VMEM_LIMIT)
    y = pl.pallas_call(
        _fft_a_kernel,
        grid=(b, n2 // ga),
        in_specs=[pl.BlockSpec((None, n1, ga, w), lambda bi, j: (bi, 0, j, 0)),
                  pl.BlockSpec((2 * n1, n1), lambda bi, j: (0, 0)),
                  pl.BlockSpec((2 * FOURIER_GROUP_DIM, 2 * FOURIER_GROUP_DIM), lambda bi, j: (0, 0))],
        out_specs=pl.BlockSpec((None, n1, 2, ga, w), lambda bi, j: (bi, 0, 0, j, 0)),
        out_shape=jax.ShapeDtypeStruct((b, n1, 2, n2, w), _F32),
        compiler_params=cparams,
        name="fft_a",
    )(u.reshape(b, n1, n2, w), cs1, m2)
    x = pl.pallas_call(
        _fft_b_kernel,
        grid=(b, n1 // gb),
        in_specs=[pl.BlockSpec((None, gb, 2 * n2, w), lambda bi, j: (bi, j, 0, 0)),
                  pl.BlockSpec((gb, n2, 2 * n2), lambda bi, j: (j, 0, 0))],
        out_specs=pl.BlockSpec((None, n2, gb, w), lambda bi, j: (bi, 0, j, 0)),
        out_shape=jax.ShapeDtypeStruct((b, n2, n1, w), _F32),
        compiler_params=cparams,
        name="fft_b",
    )(y.reshape(b, n1, 2 * n2, w), g)
    return x.reshape(b, s, w)


def _merge_kernel(x_ref, o_ref, sa_ref, f_ref, sf_ref, g_ref, wap_ref, wfp_ref, wm_ref, bm_ref,
                  wo_ref, out_ref):
    x = x_ref[...]
    d = x.shape[-1]
    ms = jnp.mean(x * x, axis=-1, keepdims=True)
    h = (x * lax.rsqrt(ms + NORM_EPS) * g_ref[...]).astype(_BF16)
    a = o_ref[...] * sa_ref[...]
    y_attn = jnp.dot(a, wap_ref[...], preferred_element_type=_F32)
    f = (f_ref[...] * sf_ref[...].astype(_F32)).astype(_BF16)
    y_four = jnp.dot(f, wfp_ref[...], preferred_element_type=_F32)
    gates = jax.nn.sigmoid(jnp.dot(h, wm_ref[...], preferred_element_type=_F32) + bm_ref[...])
    merged = (gates[:, :d] * y_attn + gates[:, d:] * y_four).astype(_BF16)
    out_ref[...] = x + jnp.dot(merged, wo_ref[...], preferred_element_type=_F32)


def _merge(x, o, sa, f, sf, norm_g, w_attn_proj, w_fourier_proj, w_merge, b_merge, w_out):
    b, s, d = x.shape
    r = min(ROW_TILE, s)
    row = lambda width: pl.BlockSpec((None, r, width), lambda bi, i: (bi, i, 0))
    full = lambda shape: pl.BlockSpec(shape, lambda bi, i: (0,) * len(shape))
    return pl.pallas_call(
        _merge_kernel,
        grid=(b, s // r),
        in_specs=[row(d), row(ATTN_WIDTH), row(ATTN_WIDTH), row(FOURIER_WIDTH), row(FOURIER_WIDTH),
                  full((1, d)), full((ATTN_WIDTH, d)), full((FOURIER_WIDTH, d)), full((d, 2 * d)),
                  full((1, 2 * d)), full((d, d))],
        out_specs=row(d),
        out_shape=jax.ShapeDtypeStruct((b, s, d), x.dtype),
        compiler_params=pltpu.CompilerParams(
            dimension_semantics=("parallel", "parallel"), vmem_limit_bytes=VMEM_LIMIT),
        name="merge",
    )(x, o, sa, f, sf, norm_g.reshape(1, d), w_attn_proj.astype(_BF16), w_fourier_proj.astype(_BF16),
      w_merge.astype(_BF16), b_merge.reshape(1, 2 * d), w_out.astype(_BF16))


def kernel(x, norm_g, w_in, q_norm_g, k_norm_g, w_attn_proj, w_fourier_proj, w_merge, b_merge, w_out):
    depth = norm_g.shape[0]
    for l in range(depth):
        q, k, vt, sa, u, sf = _proj(x, norm_g[l], w_in[l], q_norm_g[l], k_norm_g[l])
        o = _attention(q, k, vt)
        f = _fourier_mix(u)
        x = _merge(x, o, sa, f, sf, norm_g[l], w_attn_proj[l], w_fourier_proj[l], w_merge[l],
                   b_merge[l], w_out[l])
    return x
```
